```python
import math
import jax, jax.numpy as jnp
from jax import lax
import numpy as np

D_MODEL = 1024
BATCH = 8
SEQ = 4096
DEPTH = 4

GRID_W = 64
CTX_LEN = 256
D_MIX = 1024
D_CONV = 512
N_HEADS_DN = 4
HEAD_DK = 128
HEAD_DV = 128
D_DN = N_HEADS_DN * HEAD_DV
CONV_W = 3
CHUNK = 64
EPS = 1e-6

PROJ_SIZES = (D_CONV, D_CONV, D_CONV, D_CONV,
              N_HEADS_DN * HEAD_DK, N_HEADS_DN * HEAD_DK, D_DN, D_DN,
              2 * N_HEADS_DN, 2 * N_HEADS_DN)
PROJ_SPLITS = tuple(int(s) for s in np.cumsum(PROJ_SIZES)[:-1])
D_PROJ = sum(PROJ_SIZES)
D_QKV = 2 * N_HEADS_DN * HEAD_DK + D_DN

kernel_name = "hybrid_conv_deltanet_dit_block"


def rms_norm(x, w):
    x32 = x.astype(jnp.float32)
    y = x32 * lax.rsqrt(jnp.mean(x32 * x32, axis=-1, keepdims=True) + EPS)
    return (y * w.astype(jnp.float32)).astype(x.dtype)


def l2_normalize(t):
    t32 = t.astype(jnp.float32)
    return t32 * lax.rsqrt(jnp.sum(t32 * t32, axis=-1, keepdims=True) + EPS)


def to_scan_order(x, col_major):
    if not col_major:
        return x
    b, n, f = x.shape
    rows = n // GRID_W
    return x.reshape(b, rows, GRID_W, f).transpose(0, 2, 1, 3).reshape(b, n, f)


def from_scan_order(x, col_major):
    if not col_major:
        return x
    b, n, f = x.shape
    rows = n // GRID_W
    return x.reshape(b, GRID_W, rows, f).transpose(0, 2, 1, 3).reshape(b, n, f)


def segment_conv(x, w, seg):
    b, n, ch = x.shape
    pad = CONV_W // 2
    xs = jnp.pad(x.reshape(b, n // seg, seg, ch), ((0, 0), (0, 0), (pad, pad), (0, 0)))
    y = w[0] * xs[:, :, 0:seg]
    for j in range(1, CONV_W):
        y = y + w[j] * xs[:, :, j:j + seg]
    return y.reshape(b, n, ch)


def split_proj(p):
    return jnp.split(p, PROJ_SPLITS, axis=-1)


def gated_delta_chunked(q, k, v, beta, g, s0, with_output):
    b, l, h, dk = q.shape
    dv = v.shape[-1]
    n = l // CHUNK

    def chunks(t):
        t = t.astype(jnp.float32).reshape((b, n, CHUNK) + t.shape[2:])
        return jnp.swapaxes(t, 2, 3)

    qc, kc, vc, bc, gch = (chunks(t) for t in (q, k, v, beta, g))
    gcum = jnp.cumsum(gch, axis=-1)
    idx = jnp.arange(CHUNK)
    incl = idx[:, None] >= idx[None, :]
    strict = idx[:, None] > idx[None, :]
    decay = jnp.exp(jnp.where(incl, gcum[..., :, None] - gcum[..., None, :], -jnp.inf))
    kk = jnp.einsum('bnhtd,bnhsd->bnhts', kc, kc)
    a_mat = jnp.where(strict, bc[..., :, None] * kk * decay, 0.0)
    rhs = jnp.concatenate([vc * bc[..., None], kc * (bc * jnp.exp(gcum))[..., None]], axis=-1)
    sol = lax.linalg.triangular_solve(a_mat + jnp.eye(CHUNK, dtype=jnp.float32), rhs,
                                      left_side=True, lower=True, unit_diagonal=True)
    u0, w = sol[..., :dv], sol[..., dv:]
    g_last = gcum[..., -1]
    k_dec = kc * jnp.exp(g_last[..., None] - gcum)[..., None]
    xs = [u0, w, k_dec, g_last]
    if with_output:
        p_intra = jnp.einsum('bnhtd,bnhsd->bnhts', qc, kc) * decay
        q_dec = qc * jnp.exp(gcum)[..., None]
        xs = xs + [p_intra, q_dec]
    xs = tuple(jnp.moveaxis(t, 1, 0) for t in xs)

    def step(s, inp):
        u0_i, w_i, kd_i, gl_i = inp[:4]
        u = u0_i - jnp.einsum('bhtk,bhkv->bhtv', w_i, s)
        s_new = s * jnp.exp(gl_i)[..., None, None] + jnp.einsum('bhtk,bhtv->bhkv', kd_i, u)
        if with_output:
            p_i, qd_i = inp[4:]
            o = jnp.einsum('bhtk,bhkv->bhtv', qd_i, s) + jnp.einsum('bhts,bhsv->bhtv', p_i, u)
            return s_new, o
        return s_new, None

    s_fin, o = lax.scan(step, s0.astype(jnp.float32), xs)
    if with_output:
        o = jnp.transpose(o, (1, 0, 3, 2, 4)).reshape(b, l, h, dv)
    return o, s_fin


def deltanet_inputs(q, k, v, beta_logit, alpha_in, conv_qkv, a_log, dt_bias, seg):
    b, n, _ = q.shape
    qkv = jax.nn.silu(segment_conv(jnp.concatenate([q, k, v], axis=-1), conv_qkv, seg))
    q, k, v = jnp.split(qkv, [N_HEADS_DN * HEAD_DK, 2 * N_HEADS_DN * HEAD_DK], axis=-1)
    q = l2_normalize(q.reshape(b, n, N_HEADS_DN, HEAD_DK)) * (HEAD_DK ** -0.5)
    k = l2_normalize(k.reshape(b, n, N_HEADS_DN, HEAD_DK))
    v = v.reshape(b, n, N_HEADS_DN, HEAD_DV).astype(jnp.float32)
    beta = jax.nn.sigmoid(beta_logit.reshape(b, n, 2, N_HEADS_DN).astype(jnp.float32))
    g = -jnp.exp(a_log.astype(jnp.float32)) * jax.nn.softplus(
        alpha_in.reshape(b, n, 2, N_HEADS_DN).astype(jnp.float32) + dt_bias.astype(jnp.float32))
    return q, k, v, beta, g


def bidirectional_gdn(dl, dc, ctx_out):
    ql, kl, vl, bl, gl = dl
    qc, kc, vc, bc, gc = dc
    s0 = jnp.zeros((ql.shape[0], N_HEADS_DN, HEAD_DK, HEAD_DV), jnp.float32)

    def flip(t):
        return jnp.flip(t, axis=1)

    oc_f, sc_f = gated_delta_chunked(qc, kc, vc, bc[:, :, 0], gc[:, :, 0], s0, ctx_out)
    ol_f, _ = gated_delta_chunked(ql, kl, vl, bl[:, :, 0], gl[:, :, 0], sc_f, True)
    oc_b, sc_b = gated_delta_chunked(flip(qc), flip(kc), flip(vc), flip(bc[:, :, 1]),
                                     flip(gc[:, :, 1]), s0, ctx_out)
    ol_b, _ = gated_delta_chunked(flip(ql), flip(kl), flip(vl), flip(bl[:, :, 1]),
                                  flip(gl[:, :, 1]), sc_b, True)
    ol = ol_f + flip(ol_b)
    oc = oc_f + flip(oc_b) if ctx_out else None
    return ol, oc


def branch_outputs(p, o_dn, conv_a, gdn_norm, seg):
    xa, bg, cg, za = p[0], p[1], p[2], p[3]
    zb = p[7]
    b, n, _ = za.shape
    ya = bg * segment_conv(cg * xa, conv_a, seg) * jax.nn.silu(za)
    yb = rms_norm(o_dn, gdn_norm) * jax.nn.silu(
        zb.reshape(b, n, N_HEADS_DN, HEAD_DV).astype(jnp.float32))
    return jnp.concatenate([ya, yb.reshape(b, n, D_DN).astype(ya.dtype)], axis=-1)


def hybrid_layer(xl, xc, c, c_ctx, norm_w, w_mod, b_mod, w_in, conv_a, conv_qkv,
                 a_log, dt_bias, gdn_norm, w_out, col_major, ctx_out):
    n = xl.shape[1]
    rows = n // GRID_W
    seg_l = rows if col_major else GRID_W
    seg_c = xc.shape[1]
    mod_l = jax.nn.silu(c) @ w_mod + b_mod
    shift_l, scale_l, gate_l = jnp.split(mod_l[:, None, :], 3, axis=-1)
    mod_c = jax.nn.silu(c_ctx) @ w_mod + b_mod
    shift_c, scale_c, gate_c = jnp.split(mod_c, 3, axis=-1)
    hl = rms_norm(xl, norm_w) * (1.0 + scale_l) + shift_l
    hc = rms_norm(xc, norm_w) * (1.0 + scale_c) + shift_c
    pl = split_proj(to_scan_order(hl, col_major) @ w_in)
    pc = split_proj(hc @ w_in)
    dl = deltanet_inputs(pl[4], pl[5], pl[6], pl[8], pl[9], conv_qkv, a_log, dt_bias, seg_l)
    dc = deltanet_inputs(pc[4], pc[5], pc[6], pc[8], pc[9], conv_qkv, a_log, dt_bias, seg_c)
    ol, oc = bidirectional_gdn(dl, dc, ctx_out)
    yl = from_scan_order(branch_outputs(pl, ol, conv_a, gdn_norm, seg_l), col_major)
    xl = xl + gate_l * (yl @ w_out)
    if ctx_out:
        yc = branch_outputs(pc, oc, conv_a, gdn_norm, seg_c)
        xc = xc + gate_c * (yc @ w_out)
    return xl, xc


def setup_inputs(seed: int = 0) -> dict:
    key = jax.random.key(seed)
    ks = jax.random.split(key, 16)
    f32 = jnp.float32
    x = jax.random.normal(ks[0], (BATCH, SEQ, D_MODEL), f32)
    c = jax.random.normal(ks[1], (BATCH, D_MODEL), f32)
    ctx = jax.random.normal(ks[2], (BATCH, CTX_LEN, D_MODEL), f32)
    c_ctx = jax.random.normal(ks[3], (D_MODEL,), f32)
    norm_w = 1.0 + 0.05 * jax.random.normal(ks[4], (DEPTH, D_MODEL), f32)
    w_mod = 0.5 * D_MODEL ** -0.5 * jax.random.normal(ks[5], (DEPTH, D_MODEL, 3 * D_MODEL), f32)
    b_mod = 0.02 * jax.random.normal(ks[6], (DEPTH, 3 * D_MODEL), f32)
    w_in = D_MODEL ** -0.5 * jax.random.normal(ks[7], (DEPTH, D_MODEL, D_PROJ), f32)
    conv_a = CONV_W ** -0.5 * jax.random.normal(ks[8], (DEPTH, CONV_W, D_CONV), f32)
    conv_qkv = CONV_W ** -0.5 * jax.random.normal(ks[9], (DEPTH, CONV_W, D_QKV), f32)
    a_log = jnp.log(jax.random.uniform(ks[10], (DEPTH, 2, N_HEADS_DN), f32, 1.0, 16.0))
    dt = jnp.exp(jax.random.uniform(ks[11], (DEPTH, 2, N_HEADS_DN), f32)
                 * (math.log(0.1) - math.log(0.001)) + math.log(0.001))
    dt_bias = dt + jnp.log(-jnp.expm1(-dt))
    gdn_norm = 1.0 + 0.05 * jax.random.normal(ks[12], (DEPTH, HEAD_DV), f32)
    w_out = D_MIX ** -0.5 * jax.random.normal(ks[13], (DEPTH, D_MIX, D_MODEL), f32)
    final_norm = 1.0 + 0.05 * jax.random.normal(ks[14], (D_MODEL,), f32)
    return {"x": x, "c": c, "ctx": ctx, "c_ctx": c_ctx, "norm_w": norm_w,
            "w_mod": w_mod, "b_mod": b_mod, "w_in": w_in, "conv_a": conv_a,
            "conv_qkv": conv_qkv, "a_log": a_log, "dt_bias": dt_bias,
            "gdn_norm": gdn_norm, "w_out": w_out, "final_norm": final_norm}


def reference(x, c, ctx, c_ctx, norm_w, w_mod, b_mod, w_in, conv_a, conv_qkv,
              a_log, dt_bias, gdn_norm, w_out, final_norm):
    xl, xc = x, ctx
    for i in range(DEPTH):
        xl, xc = hybrid_layer(xl, xc, c, c_ctx, norm_w[i], w_mod[i], b_mod[i], w_in[i],
                              conv_a[i], conv_qkv[i], a_log[i], dt_bias[i], gdn_norm[i],
                              w_out[i], col_major=(i % 2 == 1), ctx_out=(i < DEPTH - 1))
    return rms_norm(xl, final_norm)
```

```python
import functools

import jax
import jax.numpy as jnp
from jax import lax
from jax.experimental import pallas as pl
from jax.experimental.pallas import tpu as pltpu

F32 = jnp.float32
BF16 = jnp.bfloat16
HIGHEST = lax.Precision.HIGHEST

GRID_W = 64
CHUNK = 64
D_CONV = 512
N_HEADS = 4
HEAD_D = 128
D_DN = N_HEADS * HEAD_D
D_MAIN = 4 * D_CONV + 4 * D_DN
N_GATE = 2 * N_HEADS
LANES = 128
EPS = 1e-6
VMEM_LIMIT_BYTES = 56 * 1024 * 1024


def _silu(t):
    return t * jax.nn.sigmoid(t)


def _bdot(a, b):
    return jnp.dot(a.astype(BF16), b.astype(BF16), preferred_element_type=F32)


def _bdot_nt(a, b):
    return lax.dot_general(a.astype(BF16), b.astype(BF16), (((1,), (1,)), ((), ())),
                           preferred_element_type=F32)


def _bdot_tn(a, b):
    return lax.dot_general(a.astype(BF16), b.astype(BF16), (((0,), (0,)), ((), ())),
                           preferred_element_type=F32)


def _mod_kernel(c_ref, w_ref, b_ref, o_ref):
    s = _silu(c_ref[...])
    o_ref[0] = jnp.dot(s, w_ref[0], precision=HIGHEST, preferred_element_type=F32) + b_ref[0]


def _modulation(c_rows, w_mod, b_mod):
    depth, d, d3 = w_mod.shape
    rows = c_rows.shape[0]
    return pl.pallas_call(
        _mod_kernel,
        grid=(depth, d3 // d),
        in_specs=[
            pl.BlockSpec((rows, d), lambda l, j: (0, 0)),
            pl.BlockSpec((1, d, d), lambda l, j: (l, 0, j)),
            pl.BlockSpec((1, 1, d), lambda l, j: (l, 0, j)),
        ],
        out_specs=pl.BlockSpec((1, rows, d), lambda l, j: (l, 0, j)),
        out_shape=jax.ShapeDtypeStruct((depth, rows, d3), F32),
        compiler_params=pltpu.CompilerParams(dimension_semantics=("arbitrary", "arbitrary")),
        name="adaln_mod",
    )(c_rows, w_mod, b_mod.reshape(depth, 1, d3))


def _pre_kernel(x_ref, sh_ref, sc_ref, nw_ref, w_ref, wt_ref, ca_ref, cq_ref, al_ref, dtb_ref,
                q_ref, k_ref, v_ref, ya_ref, gz_ref, aux_ref, auxt_ref, *, seg):
    x = x_ref[0]
    tm = x.shape[0]
    h = x * lax.rsqrt(jnp.mean(x * x, axis=-1, keepdims=True) + EPS) * nw_ref[...]
    h = h * (1.0 + sc_ref[0]) + sh_ref[0]
    hb = h.astype(BF16)

    pos = lax.broadcasted_iota(jnp.int32, (tm, 1), 0) & (seg - 1)
    first = pos == 0
    last = pos == seg - 1

    def conv3(u, w_ref_):
        prev = jnp.where(first, 0.0, pltpu.roll(u, 1, 0))
        nxt = jnp.where(last, 0.0, pltpu.roll(u, tm - 1, 0))
        return w_ref_[0:1, :] * prev + w_ref_[1:2, :] * u + w_ref_[2:3, :] * nxt

    pa = jnp.dot(hb, w_ref[:, 0:4 * D_CONV], preferred_element_type=F32)
    xa, bg = pa[:, 0:D_CONV], pa[:, D_CONV:2 * D_CONV]
    cg, za = pa[:, 2 * D_CONV:3 * D_CONV], pa[:, 3 * D_CONV:4 * D_CONV]
    ya_ref[0] = (bg * conv3(cg * xa, ca_ref) * _silu(za)).astype(BF16)

    o0 = 4 * D_CONV
    cq = _silu(conv3(jnp.dot(hb, w_ref[:, o0:o0 + 3 * D_DN], preferred_element_type=F32), cq_ref))
    for hd in range(N_HEADS):
        lo = hd * HEAD_D
        qh = cq[:, lo:lo + HEAD_D]
        kh = cq[:, D_DN + lo:D_DN + lo + HEAD_D]
        qn = qh * lax.rsqrt(jnp.sum(qh * qh, axis=-1, keepdims=True) + EPS) * (HEAD_D ** -0.5)
        kn = kh * lax.rsqrt(jnp.sum(kh * kh, axis=-1, keepdims=True) + EPS)
        q_ref[0, :, lo:lo + HEAD_D] = qn.astype(BF16)
        k_ref[0, :, lo:lo + HEAD_D] = kn.astype(BF16)
    v_ref[0] = cq[:, 2 * D_DN:3 * D_DN].astype(BF16)

    o1 = o0 + 3 * D_DN
    gz_ref[0] = _silu(jnp.dot(hb, w_ref[:, o1:o1 + D_DN], preferred_element_type=F32)).astype(BF16)

    t = jnp.dot(hb, wt_ref[...], preferred_element_type=F32)
    lane = lax.broadcasted_iota(jnp.int32, (CHUNK, LANES), 1)
    beta = jax.nn.sigmoid(t)
    g = -jnp.exp(al_ref[...]) * jax.nn.softplus(t + dtb_ref[...])
    g = jnp.where((lane[0:1] >= N_GATE) & (lane[0:1] < 2 * N_GATE), g, 0.0)
    r = lax.broadcasted_iota(jnp.int32, (CHUNK, CHUNK), 0)
    s = lax.broadcasted_iota(jnp.int32, (CHUNK, CHUNK), 1)
    tri_f = (s <= r).astype(F32)
    tri_b = (s >= r).astype(F32)
    bwd_lane = lane >= N_GATE + N_HEADS
    for j in range(tm // CHUNK):
        gj = g[j * CHUNK:(j + 1) * CHUNK]
        cf = jnp.dot(tri_f, gj, precision=HIGHEST, preferred_element_type=F32)
        cb = jnp.dot(tri_b, gj, precision=HIGHEST, preferred_element_type=F32)
        aj = jnp.where(lane < N_GATE, beta[j * CHUNK:(j + 1) * CHUNK], jnp.where(bwd_lane, cb, cf))
        aux_ref[0, j * CHUNK:(j + 1) * CHUNK, :] = aj
        auxt_ref[0, j] = aj.T[0:2 * N_GATE, :]


def _pre(x, shift, scale, norm_w, w_main, w_tail, conv_a, conv_qkv, a_log_l, dtb_l, *, seg, tm):
    b, t, d = x.shape
    nt = t // tm
    row = lambda bi, i: (bi, i, 0)
    const2 = lambda bi, i: (0, 0)
    per_b = lambda bi, i: (bi, 0, 0)
    outs = [jax.ShapeDtypeStruct((b, t, D_DN), BF16)] * 3 + [
        jax.ShapeDtypeStruct((b, t, D_CONV), BF16),
        jax.ShapeDtypeStruct((b, t, D_DN), BF16),
        jax.ShapeDtypeStruct((b, t, LANES), F32),
        jax.ShapeDtypeStruct((b, t // CHUNK, 2 * N_GATE, CHUNK), F32),
    ]
    out_specs = [pl.BlockSpec((1, tm, D_DN), row)] * 3 + [
        pl.BlockSpec((1, tm, D_CONV), row),
        pl.BlockSpec((1, tm, D_DN), row),
        pl.BlockSpec((1, tm, LANES), row),
        pl.BlockSpec((1, tm // CHUNK, 2 * N_GATE, CHUNK), lambda bi, i: (bi, i, 0, 0)),
    ]
    return pl.pallas_call(
        functools.partial(_pre_kernel, seg=seg),
        grid=(b, nt),
        in_specs=[
            pl.BlockSpec((1, tm, d), row),
            pl.BlockSpec((1, 1, d), per_b),
            pl.BlockSpec((1, 1, d), per_b),
            pl.BlockSpec((1, d), const2),
            pl.BlockSpec(w_main.shape, const2),
            pl.BlockSpec(w_tail.shape, const2),
            pl.BlockSpec(conv_a.shape, const2),
            pl.BlockSpec(conv_qkv.shape, const2),
            pl.BlockSpec((1, LANES), const2),
            pl.BlockSpec((1, LANES), const2),
        ],
        out_specs=out_specs,
        out_shape=outs,
        compiler_params=pltpu.CompilerParams(
            dimension_semantics=("arbitrary", "arbitrary"), vmem_limit_bytes=VMEM_LIMIT_BYTES),
        name=f"pre_seg{seg}",
    )(x, shift, scale, norm_w, w_main, w_tail, conv_a, conv_qkv, a_log_l, dtb_l)


def _scan_kernel(*refs, nc, has_s0, emit_o, emit_s):
    refs = list(refs)
    dir_in = [refs[0:5], refs[5:10]]
    pos = 10
    s0_ref = None
    if has_s0:
        s0_ref = refs[pos]
        pos += 1
    o_refs = [None, None]
    if emit_o:
        o_refs = refs[pos:pos + 2]
        pos += 2
    sfin_ref = None
    if emit_s:
        sfin_ref = refs[pos]
        pos += 1
    s_ref = refs[pos]

    i = pl.program_id(1)

    @pl.when(i == 0)
    def _():
        if has_s0:
            s_ref[...] = s0_ref[0]
        else:
            s_ref[...] = jnp.zeros(s_ref.shape, F32)

    tt = lax.broadcasted_iota(jnp.int32, (CHUNK, CHUNK), 0)
    ss = lax.broadcasted_iota(jnp.int32, (CHUNK, CHUNK), 1)
    eye = (tt == ss).astype(F32)

    for d in range(2):
        q_ref, k_ref, v_ref, aux_ref, auxt_ref = dir_in[d]
        aux = aux_ref[0]
        auxt = auxt_ref[0, 0]
        if d == 0:
            strict, incl, end = tt > ss, tt >= ss, CHUNK - 1
        else:
            strict, incl, end = tt < ss, tt <= ss, 0
        for hd in range(N_HEADS):
            e = d * N_HEADS + hd
            lo = hd * HEAD_D
            qh = q_ref[0, :, lo:lo + HEAD_D].astype(F32)
            kh = k_ref[0, :, lo:lo + HEAD_D].astype(F32)
            vh = v_ref[0, :, lo:lo + HEAD_D].astype(F32)
            bcol = aux[:, e:e + 1]
            gcol = aux[:, N_GATE + e:N_GATE + e + 1]
            grow = auxt[N_GATE + e:N_GATE + e + 1, :]
            glast = gcol[end:end + 1, :]
            decay = jnp.exp(jnp.where(incl, gcol - grow, -jnp.inf))
            a_mat = jnp.where(strict, bcol * _bdot_nt(kh, kh) * decay, 0.0)
            pw = -a_mat
            tinv = eye + pw
            for _ in range(5):
                pw = _bdot(pw, pw)
                tinv = tinv + _bdot(tinv, pw)
            egc = jnp.exp(gcol)
            rhs = jnp.concatenate([vh * bcol, kh * (bcol * egc)], axis=1)
            sol = _bdot(tinv, rhs)
            u0, w = sol[:, 0:HEAD_D], sol[:, HEAD_D:2 * HEAD_D]
            s_h = s_ref[e]
            u = u0 - _bdot(w, s_h)
            if emit_o:
                p_intra = jnp.where(incl, _bdot_nt(qh, kh) * decay, 0.0)
                o = _bdot(qh * egc, s_h) + _bdot(p_intra, u)
                o_refs[d][0, :, lo:lo + HEAD_D] = o
            k_dec = kh * jnp.exp(glast - gcol)
            s_ref[e] = s_h * jnp.exp(glast) + _bdot_tn(k_dec, u)

    if emit_s:
        @pl.when(i == nc - 1)
        def _():
            sfin_ref[0] = s_ref[...]


def _scan(q, k, v, aux, auxt, s0, *, emit_o, emit_s):
    b, t, _ = q.shape
    nc = t // CHUNK
    fwd = lambda bi, i: (bi, i, 0)
    bwd = lambda bi, i: (bi, nc - 1 - i, 0)
    fwd4 = lambda bi, i: (bi, i, 0, 0)
    bwd4 = lambda bi, i: (bi, nc - 1 - i, 0, 0)
    per_b4 = lambda bi, i: (bi, 0, 0, 0)

    def dir_specs(m3, m4):
        return [pl.BlockSpec((1, CHUNK, D_DN), m3)] * 3 + [
            pl.BlockSpec((1, CHUNK, LANES), m3),
            pl.BlockSpec((1, 1, 2 * N_GATE, CHUNK), m4)]

    in_specs = dir_specs(fwd, fwd4) + dir_specs(bwd, bwd4)
    args = [q, k, v, aux, auxt] * 2
    s_shape = (b, N_GATE, HEAD_D, HEAD_D)
    if s0 is not None:
        in_specs.append(pl.BlockSpec((1,) + s_shape[1:], per_b4))
        args.append(s0)
    out_specs, out_shape = [], []
    if emit_o:
        out_specs += [pl.BlockSpec((1, CHUNK, D_DN), fwd), pl.BlockSpec((1, CHUNK, D_DN), bwd)]
        out_shape += [jax.ShapeDtypeStruct((b, t, D_DN), F32)] * 2
    if emit_s:
        out_specs.append(pl.BlockSpec((1,) + s_shape[1:], per_b4))
        out_shape.append(jax.ShapeDtypeStruct(s_shape, F32))
    return pl.pallas_call(
        functools.partial(_scan_kernel, nc=nc, has_s0=s0 is not None, emit_o=emit_o, emit_s=emit_s),
        grid=(b, nc),
        in_specs=in_specs,
        out_specs=out_specs,
        out_shape=out_shape,
        scratch_shapes=[pltpu.VMEM(s_shape[1:], F32)],
        compiler_params=pltpu.CompilerParams(
            dimension_semantics=("arbitrary", "arbitrary"), vmem_limit_bytes=VMEM_LIMIT_BYTES),
        name=f"scan_nc{nc}",
    )(*args)


def _post_kernel(of_ref, ob_ref, gz_ref, ya_ref, x_ref, gate_ref, gn_ref, w_ref, fn_ref, out_ref,
                 *, transpose, final):
    o = of_ref[0] + ob_ref[0]
    gz = gz_ref[0].astype(F32)
    upd = jnp.dot(ya_ref[0], w_ref[0:D_CONV, :], preferred_element_type=F32)
    for hd in range(N_HEADS):
        lo = hd * HEAD_D
        oh = o[:, lo:lo + HEAD_D]
        yh = oh * lax.rsqrt(jnp.mean(oh * oh, axis=-1, keepdims=True) + EPS) * gn_ref[...]
        yh = (yh * gz[:, lo:lo + HEAD_D]).astype(BF16)
        upd = upd + jnp.dot(yh, w_ref[D_CONV + lo:D_CONV + lo + HEAD_D, :], preferred_element_type=F32)
    xn = x_ref[0] + gate_ref[0] * upd
    if final:
        xn = xn * lax.rsqrt(jnp.mean(xn * xn, axis=-1, keepdims=True) + EPS) * fn_ref[...]
    if transpose:
        for j in range(xn.shape[0] // GRID_W):
            out_ref[0, :, j, :] = xn[j * GRID_W:(j + 1) * GRID_W, :]
    else:
        out_ref[0] = xn


def _post(o_f, o_b, gz, ya, x, gate, gdn_norm_l, w_out_l, final_norm, *, tm, transpose, final):
    b, t, d = x.shape
    row = lambda bi, i: (bi, i, 0)
    const2 = lambda bi, i: (0, 0)
    per_b = lambda bi, i: (bi, 0, 0)
    if transpose:
        lines = tm // GRID_W
        out_shape = jax.ShapeDtypeStruct((b, GRID_W, t // GRID_W, d), F32)
        out_spec = pl.BlockSpec((1, GRID_W, lines, d), lambda bi, i: (bi, 0, i, 0))
    else:
        out_shape = jax.ShapeDtypeStruct((b, t, d), F32)
        out_spec = pl.BlockSpec((1, tm, d), row)
    out = pl.pallas_call(
        functools.partial(_post_kernel, transpose=transpose, final=final),
        grid=(b, t // tm),
        in_specs=[
            pl.BlockSpec((1, tm, D_DN), row),
            pl.BlockSpec((1, tm, D_DN), row),
            pl.BlockSpec((1, tm, D_DN), row),
            pl.BlockSpec((1, tm, D_CONV), row),
            pl.BlockSpec((1, tm, d), row),
            pl.BlockSpec((1, 1, d), per_b),
            pl.BlockSpec((1, HEAD_D), const2),
            pl.BlockSpec(w_out_l.shape, const2),
            pl.BlockSpec((1, d), const2),
        ],
        out_specs=out_spec,
        out_shape=out_shape,
        compiler_params=pltpu.CompilerParams(
            dimension_semantics=("arbitrary", "arbitrary"), vmem_limit_bytes=VMEM_LIMIT_BYTES),
        name=f"post_t{int(transpose)}f{int(final)}",
    )(o_f, o_b, gz, ya, x, gate, gdn_norm_l, w_out_l, final_norm)
    return out.reshape(b, t, d)


def kernel(x, c, ctx, c_ctx, norm_w, w_mod, b_mod, w_in, conv_a, conv_qkv, a_log, dt_bias,
           gdn_norm, w_out, final_norm):
    depth, d = norm_w.shape
    b = x.shape[0]
    assert x.shape[1] == GRID_W * GRID_W and ctx.shape[1] % CHUNK == 0

    n_rows = -(-(b + 1) // 8) * 8
    c_rows = jnp.zeros((n_rows, d), F32).at[:b].set(c).at[b].set(c_ctx)
    mod = _modulation(c_rows, w_mod, b_mod)

    w_main = w_in[:, :, :D_MAIN].astype(BF16)
    w_tail = jnp.pad(w_in[:, :, D_MAIN:], ((0, 0), (0, 0), (0, LANES - 2 * N_GATE))).astype(BF16)
    w_out_b = w_out.astype(BF16)
    pad3 = ((0, 0), (0, 8 - conv_a.shape[1]), (0, 0))
    conv_a_p = jnp.pad(conv_a, pad3)
    conv_qkv_p = jnp.pad(conv_qkv, pad3)
    gate_pad = ((0, 0), (N_GATE, LANES - 2 * N_GATE))
    a_log_p = jnp.pad(a_log.reshape(depth, N_GATE), gate_pad).reshape(depth, 1, LANES)
    dtb_p = jnp.pad(dt_bias.reshape(depth, N_GATE), gate_pad).reshape(depth, 1, LANES)
    fn = final_norm.reshape(1, d)

    xl, xc = x, ctx
    for l in range(depth):
        ml = mod[l]
        lat = [ml[:b, j * d:(j + 1) * d].reshape(b, 1, d) for j in range(3)]
        cx = [jnp.broadcast_to(ml[b, j * d:(j + 1) * d].reshape(1, 1, d), (b, 1, d)) for j in range(3)]
        nw = norm_w[l].reshape(1, d)
        gn = gdn_norm[l].reshape(1, HEAD_D)
        ctx_out = l < depth - 1
        pre_args = (nw, w_main[l], w_tail[l], conv_a_p[l], conv_qkv_p[l], a_log_p[l], dtb_p[l])

        qc, kc, vc, yac, gzc, auxc, auxtc = _pre(xc, cx[0], cx[1], *pre_args,
                                                  seg=xc.shape[1], tm=xc.shape[1])
        res = _scan(qc, kc, vc, auxc, auxtc, None, emit_o=ctx_out, emit_s=True)
        s_ctx = res[-1]
        ql, kl, vl, yal, gzl, auxl, auxtl = _pre(xl, lat[0], lat[1], *pre_args, seg=GRID_W, tm=256)
        ofl, obl = _scan(ql, kl, vl, auxl, auxtl, s_ctx, emit_o=True, emit_s=False)
        xl = _post(ofl, obl, gzl, yal, xl, lat[2], gn, w_out_b[l], fn,
                   tm=8 * GRID_W, transpose=True, final=(l == depth - 1))
        if ctx_out:
            xc = _post(res[0], res[1], gzc, yac, xc, cx[2], gn, w_out_b[l], fn,
                       tm=xc.shape[1], transpose=False, final=False)
    return xl
```

```python
import functools

import jax
import jax.numpy as jnp
from jax import lax
from jax.experimental import pallas as pl
from jax.experimental.pallas import tpu as pltpu

F32 = jnp.float32
BF16 = jnp.bfloat16
HIGHEST = lax.Precision.HIGHEST

GRID_W = 64
CHUNK = 64
D_CONV = 512
N_HEADS = 4
HEAD_D = 128
D_DN = N_HEADS * HEAD_D
D_MAIN = 4 * D_CONV + 4 * D_DN
N_GATE = 2 * N_HEADS
PACKED = N_HEADS * CHUNK
LANES = 128
EPS = 1e-6
VMEM_LIMIT_BYTES = 56 * 1024 * 1024

G_BETA, G_GCUM, G_EG, G_BEG, G_EKD, G_GAM, N_GROUPS = 0, 1, 2, 3, 4, 5, 6
BETA_GROUPS = (G_BETA, G_BEG)
ALPHA_GROUPS = (G_GCUM, G_EG, G_BEG, G_EKD, G_GAM)


def _silu(t):
    return t * jax.nn.sigmoid(t)


def _dot(a, b):
    return jnp.dot(a, b, preferred_element_type=F32)


def _dot_nt(a, b):
    return lax.dot_general(a, b, (((1,), (1,)), ((), ())), preferred_element_type=F32)


def _mod_kernel(c_ref, w_ref, b_ref, o_ref):
    s = _silu(c_ref[...])
    o_ref[0] = jnp.dot(s, w_ref[0], precision=HIGHEST, preferred_element_type=F32) + b_ref[0]


def _modulation(c_rows, w_mod, b_mod):
    depth, d, d3 = w_mod.shape
    rows = c_rows.shape[0]
    return pl.pallas_call(
        _mod_kernel,
        grid=(depth, d3 // d),
        in_specs=[
            pl.BlockSpec((rows, d), lambda l, j: (0, 0)),
            pl.BlockSpec((1, d, d), lambda l, j: (l, 0, j)),
            pl.BlockSpec((1, 1, d), lambda l, j: (l, 0, j)),
        ],
        out_specs=pl.BlockSpec((1, rows, d), lambda l, j: (l, 0, j)),
        out_shape=jax.ShapeDtypeStruct((depth, rows, d3), F32),
        compiler_params=pltpu.CompilerParams(dimension_semantics=("arbitrary", "arbitrary")),
        name="adaln_mod",
    )(c_rows, w_mod, b_mod.reshape(depth, 1, d3))


def _pre_kernel(x_ref, sh_ref, sc_ref, nw_ref, w_ref, wt_ref, ca_ref, cq_ref, al_ref, dtb_ref,
                q_ref, k_ref, v_ref, kt_ref, ya_ref, gz_ref, coa_ref, cop_ref, rv_ref, rows_ref,
                aux_ref, *, seg):
    x = x_ref[0]
    tm = x.shape[0]
    h = x * lax.rsqrt(jnp.mean(x * x, axis=-1, keepdims=True) + EPS) * nw_ref[...]
    h = h * (1.0 + sc_ref[0]) + sh_ref[0]
    hb = h.astype(BF16)

    pos = lax.broadcasted_iota(jnp.int32, (tm, 1), 0) & (seg - 1)
    first = pos == 0
    last = pos == seg - 1

    def conv3(u, w_ref_):
        prev = jnp.where(first, 0.0, pltpu.roll(u, 1, 0))
        nxt = jnp.where(last, 0.0, pltpu.roll(u, tm - 1, 0))
        return w_ref_[0:1, :] * prev + w_ref_[1:2, :] * u + w_ref_[2:3, :] * nxt

    pa = _dot(hb, w_ref[:, 0:4 * D_CONV])
    xa, bg = pa[:, 0:D_CONV], pa[:, D_CONV:2 * D_CONV]
    cg, za = pa[:, 2 * D_CONV:3 * D_CONV], pa[:, 3 * D_CONV:4 * D_CONV]
    ya_ref[0] = (bg * conv3(cg * xa, ca_ref) * _silu(za)).astype(BF16)

    o0 = 4 * D_CONV
    cq = _silu(conv3(_dot(hb, w_ref[:, o0:o0 + 3 * D_DN]), cq_ref))
    kn = []
    for hd in range(N_HEADS):
        lo = hd * HEAD_D
        qh = cq[:, lo:lo + HEAD_D]
        kh = cq[:, D_DN + lo:D_DN + lo + HEAD_D]
        qn = qh * lax.rsqrt(jnp.sum(qh * qh, axis=-1, keepdims=True) + EPS) * (HEAD_D ** -0.5)
        kn.append(kh * lax.rsqrt(jnp.sum(kh * kh, axis=-1, keepdims=True) + EPS))
        q_ref[0, :, lo:lo + HEAD_D] = qn.astype(BF16)
        k_ref[0, :, lo:lo + HEAD_D] = kn[hd].astype(BF16)
    v_ref[0] = cq[:, 2 * D_DN:3 * D_DN].astype(BF16)

    o1 = o0 + 3 * D_DN
    gz_ref[0] = _silu(_dot(hb, w_ref[:, o1:o1 + D_DN])).astype(BF16)

    t = _dot(hb, wt_ref[...])
    lane = lax.broadcasted_iota(jnp.int32, (CHUNK, LANES), 1)
    grp = lane >> 3
    bwd_lane = (lane & 7) >= N_HEADS
    beta = jax.nn.sigmoid(t[:, 0:LANES])
    g = -jnp.exp(al_ref[...]) * jax.nn.softplus(t[:, LANES:2 * LANES] + dtb_ref[...])
    g = jnp.where((grp[0:1] >= G_GCUM) & (grp[0:1] < N_GROUPS), g, 0.0)
    r = lax.broadcasted_iota(jnp.int32, (CHUNK, CHUNK), 0)
    s = lax.broadcasted_iota(jnp.int32, (CHUNK, CHUNK), 1)
    tri_f = (s <= r).astype(F32)
    tri_b = (s >= r).astype(F32)
    tok = lax.broadcasted_iota(jnp.int32, (CHUNK, LANES), 0)
    src = lane & (CHUNK - 1)
    lo_half = lane < CHUNK
    for j in range(tm // CHUNK):
        rows_j = slice(j * CHUNK, (j + 1) * CHUNK)
        gj = g[rows_j]
        cf = jnp.dot(tri_f, gj, precision=HIGHEST, preferred_element_type=F32)
        cb = jnp.dot(tri_b, gj, precision=HIGHEST, preferred_element_type=F32)
        gc = jnp.where(bwd_lane, cb, cf)
        glast = jnp.where(bwd_lane[0:1], gc[0:1], gc[CHUNK - 1:CHUNK])
        bj = beta[rows_j]
        eg = jnp.exp(gc)
        aj = jnp.where(grp == G_BETA, bj,
             jnp.where(grp == G_GCUM, gc,
             jnp.where(grp == G_EG, eg,
             jnp.where(grp == G_BEG, bj * eg,
             jnp.where(grp == G_EKD, jnp.exp(glast - gc),
             jnp.where(grp == G_GAM, jnp.broadcast_to(jnp.exp(glast), (CHUNK, LANES)), 0.0))))))
        aux_ref[0, rows_j, :] = aj
        ajt = jnp.concatenate([aj, aj], axis=0).T
        rows_ref[0, j] = ajt[8 * G_EKD:8 * (G_GAM + 1), :]
        rv_ref[0, j] = jnp.zeros(rv_ref.shape[2:], F32)
        for d in range(2):
            if d == 0:
                strict, incl = tok > src, tok >= src
            else:
                strict, incl = tok < src, tok <= src
            for m in range(N_HEADS // 2):
                e0 = d * N_HEADS + 2 * m
                e1 = e0 + 1
                cols = slice(d * PACKED + m * LANES, d * PACKED + (m + 1) * LANES)

                def col(gr):
                    return jnp.where(lo_half, aj[:, 8 * gr + e0:8 * gr + e0 + 1],
                                     aj[:, 8 * gr + e1:8 * gr + e1 + 1])

                def row(gr):
                    return jnp.where(lo_half[0:1], ajt[8 * gr + e0:8 * gr + e0 + 1, :],
                                     ajt[8 * gr + e1:8 * gr + e1 + 1, :])

                decay = jnp.exp(jnp.where(incl, col(G_GCUM) - row(G_GCUM), -jnp.inf))
                cop_ref[0, rows_j, cols] = decay.astype(BF16)
                coa_ref[0, rows_j, cols] = jnp.where(strict, col(G_BETA) * decay, 0.0).astype(BF16)
                rv_ref[0, j, 0:1, cols] = row(G_BETA)
                rv_ref[0, j, 1:2, cols] = row(G_BEG)
        for hd in range(N_HEADS):
            kt_ref[0, j, hd * HEAD_D:(hd + 1) * HEAD_D, :] = kn[hd][rows_j].T.astype(BF16)


def _pre(x, shift, scale, norm_w, w_main, w_tail, conv_a, conv_qkv, a_log_l, dtb_l, *, seg, tm):
    b, t, d = x.shape
    nc_t = tm // CHUNK
    row = lambda bi, i: (bi, i, 0)
    row4 = lambda bi, i: (bi, i, 0, 0)
    const2 = lambda bi, i: (0, 0)
    per_b = lambda bi, i: (bi, 0, 0)
    nc = t // CHUNK
    outs = [
        (jax.ShapeDtypeStruct((b, t, D_DN), BF16), pl.BlockSpec((1, tm, D_DN), row)),
        (jax.ShapeDtypeStruct((b, t, D_DN), BF16), pl.BlockSpec((1, tm, D_DN), row)),
        (jax.ShapeDtypeStruct((b, t, D_DN), BF16), pl.BlockSpec((1, tm, D_DN), row)),
        (jax.ShapeDtypeStruct((b, nc, D_DN, CHUNK), BF16), pl.BlockSpec((1, nc_t, D_DN, CHUNK), row4)),
        (jax.ShapeDtypeStruct((b, t, D_CONV), BF16), pl.BlockSpec((1, tm, D_CONV), row)),
        (jax.ShapeDtypeStruct((b, t, D_DN), BF16), pl.BlockSpec((1, tm, D_DN), row)),
        (jax.ShapeDtypeStruct((b, t, 2 * PACKED), BF16), pl.BlockSpec((1, tm, 2 * PACKED), row)),
        (jax.ShapeDtypeStruct((b, t, 2 * PACKED), BF16), pl.BlockSpec((1, tm, 2 * PACKED), row)),
        (jax.ShapeDtypeStruct((b, nc, 8, 2 * PACKED), F32), pl.BlockSpec((1, nc_t, 8, 2 * PACKED), row4)),
        (jax.ShapeDtypeStruct((b, nc, 2 * N_GATE, LANES), F32), pl.BlockSpec((1, nc_t, 2 * N_GATE, LANES), row4)),
        (jax.ShapeDtypeStruct((b, t, LANES), F32), pl.BlockSpec((1, tm, LANES), row)),
    ]
    return pl.pallas_call(
        functools.partial(_pre_kernel, seg=seg),
        grid=(b, t // tm),
        in_specs=[
            pl.BlockSpec((1, tm, d), row),
            pl.BlockSpec((1, 1, d), per_b),
            pl.BlockSpec((1, 1, d), per_b),
            pl.BlockSpec((1, d), const2),
            pl.BlockSpec(w_main.shape, const2),
            pl.BlockSpec(w_tail.shape, const2),
            pl.BlockSpec(conv_a.shape, const2),
            pl.BlockSpec(conv_qkv.shape, const2),
            pl.BlockSpec((1, LANES), const2),
            pl.BlockSpec((1, LANES), const2),
        ],
        out_specs=[o[1] for o in outs],
        out_shape=[o[0] for o in outs],
        compiler_params=pltpu.CompilerParams(
            dimension_semantics=("arbitrary", "arbitrary"), vmem_limit_bytes=VMEM_LIMIT_BYTES),
        name=f"pre_seg{seg}",
    )(x, shift, scale, norm_w, w_main, w_tail, conv_a, conv_qkv, a_log_l, dtb_l)


N_DIR_IN = 9


def _scan_kernel(*refs, nb, nc, has_s0, emit_o, emit_s):
    refs = list(refs)
    dir_in = [refs[0:N_DIR_IN], refs[N_DIR_IN:2 * N_DIR_IN]]
    pos = 2 * N_DIR_IN
    s0_ref = None
    if has_s0:
        s0_ref = refs[pos]
        pos += 1
    o_refs = [None, None]
    if emit_o:
        o_refs = refs[pos:pos + 2]
        pos += 2
    sfin_ref = None
    if emit_s:
        sfin_ref = refs[pos]
        pos += 1
    s_ref = refs[pos]

    i = pl.program_id(1)

    @pl.when(i == 0)
    def _():
        if has_s0:
            s_ref[...] = s0_ref[...]
        else:
            s_ref[...] = jnp.zeros(s_ref.shape, F32)

    sub = lax.broadcasted_iota(jnp.int32, (CHUNK, PACKED), 0)
    lan = lax.broadcasted_iota(jnp.int32, (CHUNK, PACKED), 1)
    eye_p = ((lan & (CHUNK - 1)) == sub).astype(F32)
    r4 = lax.broadcasted_iota(jnp.int32, (PACKED, PACKED), 0) // CHUNK
    c4 = lax.broadcasted_iota(jnp.int32, (PACKED, PACKED), 1) // CHUNK
    bd_mask = r4 == c4
    rk = lax.broadcasted_iota(jnp.int32, (PACKED, D_DN), 0) // CHUNK
    ck = lax.broadcasted_iota(jnp.int32, (PACKED, D_DN), 1) // HEAD_D
    hk_mask = rk == ck

    def blockdiag(xp):
        return jnp.where(bd_mask, jnp.concatenate([xp] * N_HEADS, axis=0), 0.0).astype(BF16)

    def stack_heads(xw):
        return jnp.concatenate([xw[:, hd * HEAD_D:(hd + 1) * HEAD_D] for hd in range(N_HEADS)], axis=0)

    groups = [(bi, d) for bi in range(nb) for d in range(2)]
    ng = len(groups)

    kq, qs, ks, vs = [], [], [], []
    for bi, d in groups:
        q_ref, k_ref, v_ref = dir_in[d][0:3]
        qb, kb = q_ref[bi], k_ref[bi]
        k_sel = jnp.where(hk_mask, jnp.concatenate([kb.astype(F32)] * N_HEADS, axis=0), 0.0).astype(BF16)
        kq.append(_dot_nt(jnp.concatenate([kb, qb], axis=0), k_sel))
        qs.append(qb)
        ks.append(kb)
        vs.append(v_ref[bi])
    nn, pp, tinv = [], [], []
    for gi, (bi, d) in enumerate(groups):
        coa_ref, cop_ref = dir_in[d][4:6]
        a_mat = kq[gi][0:CHUNK] * coa_ref[bi].astype(F32)
        pp.append(kq[gi][CHUNK:2 * CHUNK] * cop_ref[bi].astype(F32))
        nn.append(-a_mat)
        tinv.append(eye_p + nn[gi])
    pw = [_dot(nn[gi].astype(BF16), blockdiag(nn[gi])) for gi in range(ng)]
    for _ in range(4):
        res = [_dot(jnp.concatenate([tinv[gi], pw[gi]], axis=0).astype(BF16), blockdiag(pw[gi]))
               for gi in range(ng)]
        tinv = [tinv[gi] + res[gi][0:CHUNK] for gi in range(ng)]
        pw = [res[gi][CHUNK:2 * CHUNK] for gi in range(ng)]
    res = [_dot(tinv[gi].astype(BF16), blockdiag(pw[gi])) for gi in range(ng)]
    tinv = [tinv[gi] + res[gi] for gi in range(ng)]
    u0, w = [], []
    for gi, (bi, d) in enumerate(groups):
        rv = dir_in[d][6][bi, 0]
        u0.append(_dot(blockdiag(tinv[gi] * rv[0:1]), stack_heads(vs[gi])))
        w.append(_dot(blockdiag(tinv[gi] * rv[1:2]), stack_heads(ks[gi])))
    ws = []
    for gi, (bi, d) in enumerate(groups):
        per_head = []
        for hd in range(N_HEADS):
            e = d * N_HEADS + hd
            lhs = jnp.concatenate([w[gi][hd * CHUNK:(hd + 1) * CHUNK].astype(BF16),
                                   qs[gi][:, hd * HEAD_D:(hd + 1) * HEAD_D]], axis=0)
            per_head.append(_dot(lhs, s_ref[bi, e].astype(BF16)))
        ws.append(per_head)
    u = [(u0[gi] - jnp.concatenate([ws[gi][hd][0:CHUNK] for hd in range(N_HEADS)], axis=0)).astype(BF16)
         for gi in range(ng)]
    for gi, (bi, d) in enumerate(groups):
        kt_ref, rows_ref, aux_ref = dir_in[d][3], dir_in[d][7], dir_in[d][8]
        if emit_o:
            o_intra = _dot(blockdiag(pp[gi]), u[gi])
        rows = rows_ref[bi, 0]
        aux = aux_ref[bi]
        for hd in range(N_HEADS):
            e = d * N_HEADS + hd
            if emit_o:
                eg_col = aux[:, 8 * G_EG + e:8 * G_EG + e + 1]
                o_refs[d][bi, :, hd * HEAD_D:(hd + 1) * HEAD_D] = (
                    eg_col * ws[gi][hd][CHUNK:2 * CHUNK] + o_intra[hd * CHUNK:(hd + 1) * CHUNK])
            kd_t = (kt_ref[bi, 0, hd * HEAD_D:(hd + 1) * HEAD_D, :].astype(F32)
                    * rows[e:e + 1, 0:CHUNK]).astype(BF16)
            s_ref[bi, e] = (rows[N_GATE + e:N_GATE + e + 1, :] * s_ref[bi, e]
                            + _dot(kd_t, u[gi][hd * CHUNK:(hd + 1) * CHUNK]))

    if emit_s:
        @pl.when(i == nc - 1)
        def _():
            sfin_ref[...] = s_ref[...]


def _scan(q, k, v, kt, coa, cop, rv, rows, aux, s0, *, nb, emit_o, emit_s):
    b, t, _ = q.shape
    nc = t // CHUNK
    chunk_of = [lambda i: i, lambda i: nc - 1 - i]

    def dir_specs(d):
        c = chunk_of[d]
        m3 = lambda bi, i: (bi, c(i), 0)
        m4 = lambda bi, i: (bi, c(i), 0, 0)
        return [
            pl.BlockSpec((nb, CHUNK, D_DN), m3), pl.BlockSpec((nb, CHUNK, D_DN), m3),
            pl.BlockSpec((nb, CHUNK, D_DN), m3),
            pl.BlockSpec((nb, 1, D_DN, CHUNK), m4),
            pl.BlockSpec((nb, CHUNK, PACKED), lambda bi, i: (bi, c(i), d)),
            pl.BlockSpec((nb, CHUNK, PACKED), lambda bi, i: (bi, c(i), d)),
            pl.BlockSpec((nb, 1, 8, PACKED), lambda bi, i: (bi, c(i), 0, d)),
            pl.BlockSpec((nb, 1, 2 * N_GATE, LANES), m4),
            pl.BlockSpec((nb, CHUNK, LANES), m3),
        ]

    per_b4 = lambda bi, i: (bi, 0, 0, 0)
    in_specs = dir_specs(0) + dir_specs(1)
    args = [q, k, v, kt, coa, cop, rv, rows, aux] * 2
    s_shape = (b, N_GATE, HEAD_D, HEAD_D)
    s_block = (nb,) + s_shape[1:]
    if s0 is not None:
        in_specs.append(pl.BlockSpec(s_block, per_b4))
        args.append(s0)
    out_specs, out_shape = [], []
    if emit_o:
        for d in range(2):
            c = chunk_of[d]
            out_specs.append(pl.BlockSpec((nb, CHUNK, D_DN), lambda bi, i, c=c: (bi, c(i), 0)))
            out_shape.append(jax.ShapeDtypeStruct((b, t, D_DN), F32))
    if emit_s:
        out_specs.append(pl.BlockSpec(s_block, per_b4))
        out_shape.append(jax.ShapeDtypeStruct(s_shape, F32))
    return pl.pallas_call(
        functools.partial(_scan_kernel, nb=nb, nc=nc, has_s0=s0 is not None, emit_o=emit_o, emit_s=emit_s),
        grid=(b // nb, nc),
        in_specs=in_specs,
        out_specs=out_specs,
        out_shape=out_shape,
        scratch_shapes=[pltpu.VMEM(s_block, F32)],
        compiler_params=pltpu.CompilerParams(
            dimension_semantics=("arbitrary", "arbitrary"), vmem_limit_bytes=VMEM_LIMIT_BYTES),
        name=f"scan_nc{nc}",
    )(*args)


def _post_kernel(of_ref, ob_ref, gz_ref, ya_ref, x_ref, gate_ref, gn_ref, w_ref, fn_ref, out_ref,
                 *, transpose, final):
    o = of_ref[0] + ob_ref[0]
    gz = gz_ref[0].astype(F32)
    upd = _dot(ya_ref[0], w_ref[0:D_CONV, :])
    for hd in range(N_HEADS):
        lo = hd * HEAD_D
        oh = o[:, lo:lo + HEAD_D]
        yh = oh * lax.rsqrt(jnp.mean(oh * oh, axis=-1, keepdims=True) + EPS) * gn_ref[...]
        yh = (yh * gz[:, lo:lo + HEAD_D]).astype(BF16)
        upd = upd + _dot(yh, w_ref[D_CONV + lo:D_CONV + lo + HEAD_D, :])
    xn = x_ref[0] + gate_ref[0] * upd
    if final:
        xn = xn * lax.rsqrt(jnp.mean(xn * xn, axis=-1, keepdims=True) + EPS) * fn_ref[...]
    if transpose:
        for j in range(xn.shape[0] // GRID_W):
            out_ref[0, :, j, :] = xn[j * GRID_W:(j + 1) * GRID_W, :]
    else:
        out_ref[0] = xn


def _post(o_f, o_b, gz, ya, x, gate, gdn_norm_l, w_out_l, final_norm, *, tm, transpose, final):
    b, t, d = x.shape
    row = lambda bi, i: (bi, i, 0)
    const2 = lambda bi, i: (0, 0)
    per_b = lambda bi, i: (bi, 0, 0)
    if transpose:
        lines = tm // GRID_W
        out_shape = jax.ShapeDtypeStruct((b, GRID_W, t // GRID_W, d), F32)
        out_spec = pl.BlockSpec((1, GRID_W, lines, d), lambda bi, i: (bi, 0, i, 0))
    else:
        out_shape = jax.ShapeDtypeStruct((b, t, d), F32)
        out_spec = pl.BlockSpec((1, tm, d), row)
    out = pl.pallas_call(
        functools.partial(_post_kernel, transpose=transpose, final=final),
        grid=(b, t // tm),
        in_specs=[
            pl.BlockSpec((1, tm, D_DN), row),
            pl.BlockSpec((1, tm, D_DN), row),
            pl.BlockSpec((1, tm, D_DN), row),
            pl.BlockSpec((1, tm, D_CONV), row),
            pl.BlockSpec((1, tm, d), row),
            pl.BlockSpec((1, 1, d), per_b),
            pl.BlockSpec((1, HEAD_D), const2),
            pl.BlockSpec(w_out_l.shape, const2),
            pl.BlockSpec((1, d), const2),
        ],
        out_specs=out_spec,
        out_shape=out_shape,
        compiler_params=pltpu.CompilerParams(
            dimension_semantics=("arbitrary", "arbitrary"), vmem_limit_bytes=VMEM_LIMIT_BYTES),
        name=f"post_t{int(transpose)}f{int(final)}",
    )(o_f, o_b, gz, ya, x, gate, gdn_norm_l, w_out_l, final_norm)
    return out.reshape(b, t, d)


def _gate_lanes(vals, groups):
    depth = vals.shape[0]
    out = jnp.zeros((depth, LANES), F32)
    for gr in groups:
        out = out.at[:, 8 * gr:8 * gr + N_GATE].set(vals)
    return out.reshape(depth, 1, LANES)


def kernel(x, c, ctx, c_ctx, norm_w, w_mod, b_mod, w_in, conv_a, conv_qkv, a_log, dt_bias,
           gdn_norm, w_out, final_norm):
    depth, d = norm_w.shape
    b = x.shape[0]
    ctx_len = ctx.shape[1]
    assert x.shape[1] == GRID_W * GRID_W and ctx_len % CHUNK == 0 and ctx_len & (ctx_len - 1) == 0
    nb = 2 if b % 2 == 0 else 1

    n_rows = -(-(b + 1) // 8) * 8
    c_rows = jnp.zeros((n_rows, d), F32).at[:b].set(c).at[b].set(c_ctx)
    mod = _modulation(c_rows, w_mod, b_mod)

    w_main = w_in[:, :, :D_MAIN].astype(BF16)
    w_beta = w_in[:, :, D_MAIN:D_MAIN + N_GATE]
    w_alpha = w_in[:, :, D_MAIN + N_GATE:D_MAIN + 2 * N_GATE]
    w_tail = jnp.zeros((depth, d, 2 * LANES), F32)
    for gr in BETA_GROUPS:
        w_tail = w_tail.at[:, :, 8 * gr:8 * gr + N_GATE].set(w_beta)
    for gr in ALPHA_GROUPS:
        w_tail = w_tail.at[:, :, LANES + 8 * gr:LANES + 8 * gr + N_GATE].set(w_alpha)
    w_tail = w_tail.astype(BF16)
    a_log_p = _gate_lanes(a_log.reshape(depth, N_GATE), ALPHA_GROUPS)
    dtb_p = _gate_lanes(dt_bias.reshape(depth, N_GATE), ALPHA_GROUPS)
    w_out_b = w_out.astype(BF16)
    pad3 = ((0, 0), (0, 8 - conv_a.shape[1]), (0, 0))
    conv_a_p = jnp.pad(conv_a, pad3)
    conv_qkv_p = jnp.pad(conv_qkv, pad3)
    fn = final_norm.reshape(1, d)

    xl, xc = x, ctx
    for l in range(depth):
        ml = mod[l]
        lat = [ml[:b, j * d:(j + 1) * d].reshape(b, 1, d) for j in range(3)]
        cx = [jnp.broadcast_to(ml[b, j * d:(j + 1) * d].reshape(1, 1, d), (b, 1, d)) for j in range(3)]
        nw = norm_w[l].reshape(1, d)
        gn = gdn_norm[l].reshape(1, HEAD_D)
        ctx_out = l < depth - 1
        pre_args = (nw, w_main[l], w_tail[l], conv_a_p[l], conv_qkv_p[l], a_log_p[l], dtb_p[l])

        qc, kc, vc, ktc, yac, gzc, *gates_c = _pre(xc, cx[0], cx[1], *pre_args, seg=ctx_len, tm=ctx_len)
        res = _scan(qc, kc, vc, ktc, *gates_c, None, nb=nb, emit_o=ctx_out, emit_s=True)
        s_ctx = res[-1]
        ql, kl, vl, ktl, yal, gzl, *gates_l = _pre(xl, lat[0], lat[1], *pre_args, seg=GRID_W, tm=256)
        ofl, obl = _scan(ql, kl, vl, ktl, *gates_l, s_ctx, nb=nb, emit_o=True, emit_s=False)
        xl = _post(ofl, obl, gzl, yal, xl, lat[2], gn, w_out_b[l], fn,
                   tm=8 * GRID_W, transpose=True, final=(l == depth - 1))
        if ctx_out:
            xc = _post(res[0], res[1], gzc, yac, xc, cx[2], gn, w_out_b[l], fn,
                       tm=ctx_len, transpose=False, final=False)
    return xl
```

```python
import functools

import jax
import jax.numpy as jnp
from jax import lax
from jax.experimental import pallas as pl
from jax.experimental.pallas import tpu as pltpu

F32 = jnp.float32
BF16 = jnp.bfloat16
HIGHEST = lax.Precision.HIGHEST

GRID_W = 64
CHUNK = 64
D_CONV = 512
N_HEADS = 4
HEAD_D = 128
D_DN = N_HEADS * HEAD_D
D_MAIN = 4 * D_CONV + 4 * D_DN
N_GATE = 2 * N_HEADS
PACKED = N_HEADS * CHUNK
CHUNK_LOG = 6
BASE_LOG = 3
LANES = 128
EPS = 1e-6
VMEM_LIMIT_BYTES = 56 * 1024 * 1024

G_BETA, G_GCUM, G_EG, G_BEG, G_EKD, G_GAM, N_GROUPS = 0, 1, 2, 3, 4, 5, 6
BETA_GROUPS = (G_BETA, G_BEG)
ALPHA_GROUPS = (G_GCUM, G_EG, G_BEG, G_EKD, G_GAM)


def _silu(t):
    return t * jax.nn.sigmoid(t)


def _dot(a, b):
    return jnp.dot(a, b, preferred_element_type=F32)


def _dot_nt(a, b):
    return lax.dot_general(a, b, (((1,), (1,)), ((), ())), preferred_element_type=F32)


def _mod_kernel(c_ref, w_ref, b_ref, o_ref):
    s = _silu(c_ref[...])
    o_ref[0] = jnp.dot(s, w_ref[0], precision=HIGHEST, preferred_element_type=F32) + b_ref[0]


def _modulation(c_rows, w_mod, b_mod):
    depth, d, d3 = w_mod.shape
    rows = c_rows.shape[0]
    return pl.pallas_call(
        _mod_kernel,
        grid=(depth, d3 // d),
        in_specs=[
            pl.BlockSpec((rows, d), lambda l, j: (0, 0)),
            pl.BlockSpec((1, d, d), lambda l, j: (l, 0, j)),
            pl.BlockSpec((1, 1, d), lambda l, j: (l, 0, j)),
        ],
        out_specs=pl.BlockSpec((1, rows, d), lambda l, j: (l, 0, j)),
        out_shape=jax.ShapeDtypeStruct((depth, rows, d3), F32),
        compiler_params=pltpu.CompilerParams(dimension_semantics=("arbitrary", "arbitrary")),
        name="adaln_mod",
    )(c_rows, w_mod, b_mod.reshape(depth, 1, d3))


def _pre_kernel(x_ref, sh_ref, sc_ref, nw_ref, w_ref, wt_ref, ca_ref, cq_ref, al_ref, dtb_ref,
                q_ref, k_ref, v_ref, kt_ref, ya_ref, gz_ref, coa_ref, cop_ref, rv_ref, gm_ref,
                aux_ref, *, seg):
    x = x_ref[0]
    tm = x.shape[0]
    h = x * lax.rsqrt(jnp.mean(x * x, axis=-1, keepdims=True) + EPS) * nw_ref[...]
    h = h * (1.0 + sc_ref[0]) + sh_ref[0]
    hb = h.astype(BF16)

    pos = lax.broadcasted_iota(jnp.int32, (tm, 1), 0) & (seg - 1)
    first = pos == 0
    last = pos == seg - 1

    def conv3(u, w_ref_):
        prev = jnp.where(first, 0.0, pltpu.roll(u, 1, 0))
        nxt = jnp.where(last, 0.0, pltpu.roll(u, tm - 1, 0))
        return w_ref_[0:1, :] * prev + w_ref_[1:2, :] * u + w_ref_[2:3, :] * nxt

    pa = _dot(hb, w_ref[:, 0:4 * D_CONV])
    xa, bg = pa[:, 0:D_CONV], pa[:, D_CONV:2 * D_CONV]
    cg, za = pa[:, 2 * D_CONV:3 * D_CONV], pa[:, 3 * D_CONV:4 * D_CONV]
    ya_ref[0] = (bg * conv3(cg * xa, ca_ref) * _silu(za)).astype(BF16)

    o0 = 4 * D_CONV
    cq = _silu(conv3(_dot(hb, w_ref[:, o0:o0 + 3 * D_DN]), cq_ref))
    kn = []
    for hd in range(N_HEADS):
        lo = hd * HEAD_D
        qh = cq[:, lo:lo + HEAD_D]
        kh = cq[:, D_DN + lo:D_DN + lo + HEAD_D]
        qn = qh * lax.rsqrt(jnp.sum(qh * qh, axis=-1, keepdims=True) + EPS) * (HEAD_D ** -0.5)
        kn.append(kh * lax.rsqrt(jnp.sum(kh * kh, axis=-1, keepdims=True) + EPS))
        q_ref[0, :, lo:lo + HEAD_D] = qn.astype(BF16)
        k_ref[0, :, lo:lo + HEAD_D] = kn[hd].astype(BF16)
    v_ref[0] = cq[:, 2 * D_DN:3 * D_DN].astype(BF16)

    o1 = o0 + 3 * D_DN
    gz_ref[0] = _silu(_dot(hb, w_ref[:, o1:o1 + D_DN])).astype(BF16)

    t = _dot(hb, wt_ref[...])
    lane = lax.broadcasted_iota(jnp.int32, (CHUNK, LANES), 1)
    grp = lane >> 3
    bwd_lane = (lane & 7) >= N_HEADS
    beta = jax.nn.sigmoid(t[:, 0:LANES])
    g = -jnp.exp(al_ref[...]) * jax.nn.softplus(t[:, LANES:2 * LANES] + dtb_ref[...])
    g = jnp.where((grp[0:1] >= G_GCUM) & (grp[0:1] < N_GROUPS), g, 0.0)
    r = lax.broadcasted_iota(jnp.int32, (CHUNK, CHUNK), 0)
    s = lax.broadcasted_iota(jnp.int32, (CHUNK, CHUNK), 1)
    tri_f = (s <= r).astype(F32)
    tri_b = (s >= r).astype(F32)
    tok = lax.broadcasted_iota(jnp.int32, (CHUNK, LANES), 0)
    src = lane & (CHUNK - 1)
    lo_half = lane < CHUNK
    lo_half2 = lax.broadcasted_iota(jnp.int32, (HEAD_D, LANES), 1) < CHUNK
    for j in range(tm // CHUNK):
        rows_j = slice(j * CHUNK, (j + 1) * CHUNK)
        gj = g[rows_j]
        cf = jnp.dot(tri_f, gj, precision=HIGHEST, preferred_element_type=F32)
        cb = jnp.dot(tri_b, gj, precision=HIGHEST, preferred_element_type=F32)
        gc = jnp.where(bwd_lane, cb, cf)
        glast = jnp.where(bwd_lane[0:1], gc[0:1], gc[CHUNK - 1:CHUNK])
        bj = beta[rows_j]
        eg = jnp.exp(gc)
        aj = jnp.where(grp == G_BETA, bj,
             jnp.where(grp == G_GCUM, gc,
             jnp.where(grp == G_EG, eg,
             jnp.where(grp == G_BEG, bj * eg,
             jnp.where(grp == G_EKD, jnp.exp(glast - gc),
             jnp.where(grp == G_GAM, jnp.broadcast_to(jnp.exp(glast), (CHUNK, LANES)), 0.0))))))
        aux_ref[0, rows_j, :] = aj
        ajt = jnp.concatenate([aj, aj], axis=0).T
        rv_ref[0, j] = jnp.zeros(rv_ref.shape[2:], F32)
        gm_ref[0, j] = jnp.zeros(gm_ref.shape[2:], F32)
        for d in range(2):
            if d == 0:
                strict, incl = tok > src, tok >= src
            else:
                strict, incl = tok < src, tok <= src
            for m in range(N_HEADS // 2):
                e0 = d * N_HEADS + 2 * m
                e1 = e0 + 1
                cols = slice(d * PACKED + m * LANES, d * PACKED + (m + 1) * LANES)

                def col(gr):
                    return jnp.where(lo_half, aj[:, 8 * gr + e0:8 * gr + e0 + 1],
                                     aj[:, 8 * gr + e1:8 * gr + e1 + 1])

                def row(gr):
                    return jnp.where(lo_half[0:1], ajt[8 * gr + e0:8 * gr + e0 + 1, :],
                                     ajt[8 * gr + e1:8 * gr + e1 + 1, :])

                decay = jnp.exp(jnp.where(incl, col(G_GCUM) - row(G_GCUM), -jnp.inf))
                cop_ref[0, rows_j, cols] = decay.astype(BF16)
                coa_ref[0, rows_j, cols] = jnp.where(strict, col(G_BETA) * decay, 0.0).astype(BF16)
                rv_ref[0, j, 0:1, cols] = row(G_BETA)
                rv_ref[0, j, 1:2, cols] = row(G_BEG)
                rv_ref[0, j, 2:3, cols] = row(G_EKD)
            for hd in range(N_HEADS):
                e = d * N_HEADS + hd
                gm_ref[0, j, d:d + 1, hd * HEAD_D:(hd + 1) * HEAD_D] = ajt[8 * G_GAM + e:8 * G_GAM + e + 1, :]
        for m in range(N_HEADS // 2):
            kt0 = jnp.concatenate([kn[2 * m][rows_j]] * 2, axis=0).T
            kt1 = jnp.concatenate([kn[2 * m + 1][rows_j]] * 2, axis=0).T
            kt_ref[0, j, :, m * LANES:(m + 1) * LANES] = jnp.where(lo_half2, kt0, kt1).astype(BF16)


def _pre(x, shift, scale, norm_w, w_main, w_tail, conv_a, conv_qkv, a_log_l, dtb_l, *, seg, tm):
    b, t, d = x.shape
    nc_t = tm // CHUNK
    row = lambda bi, i: (bi, i, 0)
    row4 = lambda bi, i: (bi, i, 0, 0)
    const2 = lambda bi, i: (0, 0)
    per_b = lambda bi, i: (bi, 0, 0)
    nc = t // CHUNK
    outs = [
        (jax.ShapeDtypeStruct((b, t, D_DN), BF16), pl.BlockSpec((1, tm, D_DN), row)),
        (jax.ShapeDtypeStruct((b, t, D_DN), BF16), pl.BlockSpec((1, tm, D_DN), row)),
        (jax.ShapeDtypeStruct((b, t, D_DN), BF16), pl.BlockSpec((1, tm, D_DN), row)),
        (jax.ShapeDtypeStruct((b, nc, HEAD_D, PACKED), BF16), pl.BlockSpec((1, nc_t, HEAD_D, PACKED), row4)),
        (jax.ShapeDtypeStruct((b, t, D_CONV), BF16), pl.BlockSpec((1, tm, D_CONV), row)),
        (jax.ShapeDtypeStruct((b, t, D_DN), BF16), pl.BlockSpec((1, tm, D_DN), row)),
        (jax.ShapeDtypeStruct((b, t, 2 * PACKED), BF16), pl.BlockSpec((1, tm, 2 * PACKED), row)),
        (jax.ShapeDtypeStruct((b, t, 2 * PACKED), BF16), pl.BlockSpec((1, tm, 2 * PACKED), row)),
        (jax.ShapeDtypeStruct((b, nc, 8, 2 * PACKED), F32), pl.BlockSpec((1, nc_t, 8, 2 * PACKED), row4)),
        (jax.ShapeDtypeStruct((b, nc, 8, D_DN), F32), pl.BlockSpec((1, nc_t, 8, D_DN), row4)),
        (jax.ShapeDtypeStruct((b, t, LANES), F32), pl.BlockSpec((1, tm, LANES), row)),
    ]
    return pl.pallas_call(
        functools.partial(_pre_kernel, seg=seg),
        grid=(b, t // tm),
        in_specs=[
            pl.BlockSpec((1, tm, d), row),
            pl.BlockSpec((1, 1, d), per_b),
            pl.BlockSpec((1, 1, d), per_b),
            pl.BlockSpec((1, d), const2),
            pl.BlockSpec(w_main.shape, const2),
            pl.BlockSpec(w_tail.shape, const2),
            pl.BlockSpec(conv_a.shape, const2),
            pl.BlockSpec(conv_qkv.shape, const2),
            pl.BlockSpec((1, LANES), const2),
            pl.BlockSpec((1, LANES), const2),
        ],
        out_specs=[o[1] for o in outs],
        out_shape=[o[0] for o in outs],
        compiler_params=pltpu.CompilerParams(
            dimension_semantics=("arbitrary", "arbitrary"), vmem_limit_bytes=VMEM_LIMIT_BYTES),
        name=f"pre_seg{seg}",
    )(x, shift, scale, norm_w, w_main, w_tail, conv_a, conv_qkv, a_log_l, dtb_l)


N_DIR_IN = 9


def _scan_kernel(*refs, nb, nc, has_s0, emit_o, emit_s):
    refs = list(refs)
    dir_in = [refs[0:N_DIR_IN], refs[N_DIR_IN:2 * N_DIR_IN]]
    pos = 2 * N_DIR_IN
    s0_ref = None
    if has_s0:
        s0_ref = refs[pos]
        pos += 1
    o_refs = [None, None]
    if emit_o:
        o_refs = refs[pos:pos + 2]
        pos += 2
    sfin_ref = None
    if emit_s:
        sfin_ref = refs[pos]
        pos += 1
    s_ref = refs[pos]

    i = pl.program_id(1)

    @pl.when(i == 0)
    def _():
        if has_s0:
            s_ref[...] = s0_ref[...]
        else:
            s_ref[...] = jnp.zeros(s_ref.shape, F32)

    sub = lax.broadcasted_iota(jnp.int32, (CHUNK, PACKED), 0)
    lan = lax.broadcasted_iota(jnp.int32, (CHUNK, PACKED), 1)
    src = lan & (CHUNK - 1)
    eye_p = (src == sub).astype(F32)
    base_mask = (sub >> BASE_LOG) == (src >> BASE_LOG)
    off_masks = [((sub >> (ls + 1)) == (src >> (ls + 1))) & ((sub >> ls) != (src >> ls))
                 for ls in range(BASE_LOG, CHUNK_LOG)]
    r4 = lax.broadcasted_iota(jnp.int32, (PACKED, PACKED), 0) // CHUNK
    c4 = lax.broadcasted_iota(jnp.int32, (PACKED, PACKED), 1) // CHUNK
    bd_mask = r4 == c4
    rk = lax.broadcasted_iota(jnp.int32, (PACKED, D_DN), 0) // CHUNK
    ck = lax.broadcasted_iota(jnp.int32, (PACKED, D_DN), 1) // HEAD_D
    hk_mask = rk == ck
    lo_pair = lax.broadcasted_iota(jnp.int32, (HEAD_D, 2 * HEAD_D), 1) < HEAD_D

    def blockdiag(xp):
        return jnp.where(bd_mask, jnp.concatenate([xp] * N_HEADS, axis=0), 0.0).astype(BF16)

    def headdiag(xw):
        return jnp.where(hk_mask, jnp.concatenate([xw.astype(F32)] * N_HEADS, axis=0), 0.0).astype(BF16)

    groups = [(bi, d) for bi in range(nb) for d in range(2)]
    ng = len(groups)

    kq, qs, bdk, bdv = [], [], [], []
    for bi, d in groups:
        q_ref, k_ref, v_ref = dir_in[d][0:3]
        qb, kb = q_ref[bi], k_ref[bi]
        bdk.append(headdiag(kb))
        bdv.append(headdiag(v_ref[bi]))
        lhs = jnp.concatenate([kb, qb], axis=0) if emit_o else kb
        kq.append(_dot_nt(lhs, bdk[-1]))
        qs.append(qb)
    a_mat, pp = [], []
    for gi, (bi, d) in enumerate(groups):
        coa_ref, cop_ref = dir_in[d][4:6]
        a_mat.append(kq[gi][0:CHUNK] * coa_ref[bi].astype(F32))
        if emit_o:
            pp.append(kq[gi][CHUNK:2 * CHUNK] * cop_ref[bi].astype(F32))
    nb8 = [jnp.where(base_mask, -a_mat[gi], 0.0) for gi in range(ng)]
    tinv = [eye_p + nb8[gi] for gi in range(ng)]
    pw = [_dot(nb8[gi].astype(BF16), blockdiag(nb8[gi])) for gi in range(ng)]
    for _ in range(BASE_LOG - 2):
        res = [_dot(jnp.concatenate([tinv[gi], pw[gi]], axis=0).astype(BF16), blockdiag(pw[gi]))
               for gi in range(ng)]
        tinv = [tinv[gi] + res[gi][0:CHUNK] for gi in range(ng)]
        pw = [res[gi][CHUNK:2 * CHUNK] for gi in range(ng)]
    res = [_dot(tinv[gi].astype(BF16), blockdiag(pw[gi])) for gi in range(ng)]
    tinv = [tinv[gi] + res[gi] for gi in range(ng)]
    for off in off_masks:
        xs = [_dot(jnp.where(off, a_mat[gi], 0.0).astype(BF16), blockdiag(tinv[gi])) for gi in range(ng)]
        ys = [_dot(tinv[gi].astype(BF16), blockdiag(xs[gi])) for gi in range(ng)]
        tinv = [tinv[gi] - ys[gi] for gi in range(ng)]
    u0, w = [], []
    for gi, (bi, d) in enumerate(groups):
        rv = dir_in[d][6][bi, 0]
        u0.append(_dot((tinv[gi] * rv[0:1]).astype(BF16), bdv[gi]))
        w.append(_dot((tinv[gi] * rv[1:2]).astype(BF16), bdk[gi]))
    ws = []
    for gi, (bi, d) in enumerate(groups):
        s_all = s_ref[bi, d]
        per_pair = []
        for m in range(N_HEADS // 2):
            cols = slice(2 * m * HEAD_D, 2 * (m + 1) * HEAD_D)
            s_pair = s_all[:, cols]
            rhs = jnp.concatenate([jnp.where(lo_pair, s_pair, 0.0), jnp.where(lo_pair, 0.0, s_pair)],
                                  axis=0).astype(BF16)
            lhs = w[gi][:, cols].astype(BF16)
            if emit_o:
                lhs = jnp.concatenate([lhs, qs[gi][:, cols]], axis=0)
            per_pair.append(_dot(lhs, rhs))
        ws.append(per_pair)
    u = [u0[gi] - jnp.concatenate([ws[gi][m][0:CHUNK] for m in range(N_HEADS // 2)], axis=1)
         for gi in range(ng)]
    for gi, (bi, d) in enumerate(groups):
        kt_ref, gm_ref, aux_ref = dir_in[d][3], dir_in[d][7], dir_in[d][8]
        rv = dir_in[d][6][bi, 0]
        k_dec_t = (kt_ref[bi, 0].astype(F32) * rv[2:3]).astype(BF16)
        lhs = jnp.concatenate([pp[gi].astype(BF16), k_dec_t], axis=0) if emit_o else k_dec_t
        res = _dot(lhs, headdiag(u[gi]))
        if emit_o:
            aux = aux_ref[bi]
            eg_b = jnp.concatenate(
                [jnp.broadcast_to(aux[:, 8 * G_EG + d * N_HEADS + hd:8 * G_EG + d * N_HEADS + hd + 1],
                                  (CHUNK, HEAD_D)) for hd in range(N_HEADS)], axis=1)
            q_s = jnp.concatenate([ws[gi][m][CHUNK:2 * CHUNK] for m in range(N_HEADS // 2)], axis=1)
            o_refs[d][bi] = eg_b * q_s + res[0:CHUNK]
            res = res[CHUNK:]
        s_ref[bi, d] = gm_ref[bi, 0, d:d + 1, :] * s_ref[bi, d] + res

    if emit_s:
        @pl.when(i == nc - 1)
        def _():
            sfin_ref[...] = s_ref[...]


def _scan(q, k, v, kt, coa, cop, rv, gm, aux, s0, *, nb, emit_o, emit_s):
    b, t, _ = q.shape
    nc = t // CHUNK
    chunk_of = [lambda i: i, lambda i: nc - 1 - i]

    def dir_specs(d):
        c = chunk_of[d]
        m3 = lambda bi, i: (bi, c(i), 0)
        m4 = lambda bi, i: (bi, c(i), 0, 0)
        return [
            pl.BlockSpec((nb, CHUNK, D_DN), m3), pl.BlockSpec((nb, CHUNK, D_DN), m3),
            pl.BlockSpec((nb, CHUNK, D_DN), m3),
            pl.BlockSpec((nb, 1, HEAD_D, PACKED), m4),
            pl.BlockSpec((nb, CHUNK, PACKED), lambda bi, i: (bi, c(i), d)),
            pl.BlockSpec((nb, CHUNK, PACKED), lambda bi, i: (bi, c(i), d)),
            pl.BlockSpec((nb, 1, 8, PACKED), lambda bi, i: (bi, c(i), 0, d)),
            pl.BlockSpec((nb, 1, 8, D_DN), m4),
            pl.BlockSpec((nb, CHUNK, LANES), m3),
        ]

    per_b4 = lambda bi, i: (bi, 0, 0, 0)
    in_specs = dir_specs(0) + dir_specs(1)
    args = [q, k, v, kt, coa, cop, rv, gm, aux] * 2
    s_shape = (b, 2, HEAD_D, D_DN)
    s_block = (nb,) + s_shape[1:]
    if s0 is not None:
        in_specs.append(pl.BlockSpec(s_block, per_b4))
        args.append(s0)
    out_specs, out_shape = [], []
    if emit_o:
        for d in range(2):
            c = chunk_of[d]
            out_specs.append(pl.BlockSpec((nb, CHUNK, D_DN), lambda bi, i, c=c: (bi, c(i), 0)))
            out_shape.append(jax.ShapeDtypeStruct((b, t, D_DN), F32))
    if emit_s:
        out_specs.append(pl.BlockSpec(s_block, per_b4))
        out_shape.append(jax.ShapeDtypeStruct(s_shape, F32))
    return pl.pallas_call(
        functools.partial(_scan_kernel, nb=nb, nc=nc, has_s0=s0 is not None, emit_o=emit_o, emit_s=emit_s),
        grid=(b // nb, nc),
        in_specs=in_specs,
        out_specs=out_specs,
        out_shape=out_shape,
        scratch_shapes=[pltpu.VMEM(s_block, F32)],
        compiler_params=pltpu.CompilerParams(
            dimension_semantics=("arbitrary", "arbitrary"), vmem_limit_bytes=VMEM_LIMIT_BYTES),
        name=f"scan_nc{nc}",
    )(*args)


def _post_kernel(of_ref, ob_ref, gz_ref, ya_ref, x_ref, gate_ref, gn_ref, w_ref, fn_ref, out_ref,
                 *, transpose, final):
    o = of_ref[0] + ob_ref[0]
    gz = gz_ref[0].astype(F32)
    upd = _dot(ya_ref[0], w_ref[0:D_CONV, :])
    for hd in range(N_HEADS):
        lo = hd * HEAD_D
        oh = o[:, lo:lo + HEAD_D]
        yh = oh * lax.rsqrt(jnp.mean(oh * oh, axis=-1, keepdims=True) + EPS) * gn_ref[...]
        yh = (yh * gz[:, lo:lo + HEAD_D]).astype(BF16)
        upd = upd + _dot(yh, w_ref[D_CONV + lo:D_CONV + lo + HEAD_D, :])
    xn = x_ref[0] + gate_ref[0] * upd
    if final:
        xn = xn * lax.rsqrt(jnp.mean(xn * xn, axis=-1, keepdims=True) + EPS) * fn_ref[...]
    if transpose:
        for j in range(xn.shape[0] // GRID_W):
            out_ref[0, :, j, :] = xn[j * GRID_W:(j + 1) * GRID_W, :]
    else:
        out_ref[0] = xn


def _post(o_f, o_b, gz, ya, x, gate, gdn_norm_l, w_out_l, final_norm, *, tm, transpose, final):
    b, t, d = x.shape
    row = lambda bi, i: (bi, i, 0)
    const2 = lambda bi, i: (0, 0)
    per_b = lambda bi, i: (bi, 0, 0)
    if transpose:
        lines = tm // GRID_W
        out_shape = jax.ShapeDtypeStruct((b, GRID_W, t // GRID_W, d), F32)
        out_spec = pl.BlockSpec((1, GRID_W, lines, d), lambda bi, i: (bi, 0, i, 0))
    else:
        out_shape = jax.ShapeDtypeStruct((b, t, d), F32)
        out_spec = pl.BlockSpec((1, tm, d), row)
    out = pl.pallas_call(
        functools.partial(_post_kernel, transpose=transpose, final=final),
        grid=(b, t // tm),
        in_specs=[
            pl.BlockSpec((1, tm, D_DN), row),
            pl.BlockSpec((1, tm, D_DN), row),
            pl.BlockSpec((1, tm, D_DN), row),
            pl.BlockSpec((1, tm, D_CONV), row),
            pl.BlockSpec((1, tm, d), row),
            pl.BlockSpec((1, 1, d), per_b),
            pl.BlockSpec((1, HEAD_D), const2),
            pl.BlockSpec(w_out_l.shape, const2),
            pl.BlockSpec((1, d), const2),
        ],
        out_specs=out_spec,
        out_shape=out_shape,
        compiler_params=pltpu.CompilerParams(
            dimension_semantics=("arbitrary", "arbitrary"), vmem_limit_bytes=VMEM_LIMIT_BYTES),
        name=f"post_t{int(transpose)}f{int(final)}",
    )(o_f, o_b, gz, ya, x, gate, gdn_norm_l, w_out_l, final_norm)
    return out.reshape(b, t, d)


def _gate_lanes(vals, groups):
    depth = vals.shape[0]
    out = jnp.zeros((depth, LANES), F32)
    for gr in groups:
        out = out.at[:, 8 * gr:8 * gr + N_GATE].set(vals)
    return out.reshape(depth, 1, LANES)


def kernel(x, c, ctx, c_ctx, norm_w, w_mod, b_mod, w_in, conv_a, conv_qkv, a_log, dt_bias,
           gdn_norm, w_out, final_norm):
    depth, d = norm_w.shape
    b = x.shape[0]
    ctx_len = ctx.shape[1]
    assert x.shape[1] == GRID_W * GRID_W and ctx_len % CHUNK == 0 and ctx_len & (ctx_len - 1) == 0
    nb = max(n for n in (1, 2, 4, 8) if b % n == 0)

    n_rows = -(-(b + 1) // 8) * 8
    c_rows = jnp.zeros((n_rows, d), F32).at[:b].set(c).at[b].set(c_ctx)
    mod = _modulation(c_rows, w_mod, b_mod)

    w_main = w_in[:, :, :D_MAIN].astype(BF16)
    w_beta = w_in[:, :, D_MAIN:D_MAIN + N_GATE]
    w_alpha = w_in[:, :, D_MAIN + N_GATE:D_MAIN + 2 * N_GATE]
    w_tail = jnp.zeros((depth, d, 2 * LANES), F32)
    for gr in BETA_GROUPS:
        w_tail = w_tail.at[:, :, 8 * gr:8 * gr + N_GATE].set(w_beta)
    for gr in ALPHA_GROUPS:
        w_tail = w_tail.at[:, :, LANES + 8 * gr:LANES + 8 * gr + N_GATE].set(w_alpha)
    w_tail = w_tail.astype(BF16)
    a_log_p = _gate_lanes(a_log.reshape(depth, N_GATE), ALPHA_GROUPS)
    dtb_p = _gate_lanes(dt_bias.reshape(depth, N_GATE), ALPHA_GROUPS)
    w_out_b = w_out.astype(BF16)
    pad3 = ((0, 0), (0, 8 - conv_a.shape[1]), (0, 0))
    conv_a_p = jnp.pad(conv_a, pad3)
    conv_qkv_p = jnp.pad(conv_qkv, pad3)
    fn = final_norm.reshape(1, d)

    xl, xc = x, ctx
    for l in range(depth):
        ml = mod[l]
        lat = [ml[:b, j * d:(j + 1) * d].reshape(b, 1, d) for j in range(3)]
        cx = [jnp.broadcast_to(ml[b, j * d:(j + 1) * d].reshape(1, 1, d), (b, 1, d)) for j in range(3)]
        nw = norm_w[l].reshape(1, d)
        gn = gdn_norm[l].reshape(1, HEAD_D)
        ctx_out = l < depth - 1
        pre_args = (nw, w_main[l], w_tail[l], conv_a_p[l], conv_qkv_p[l], a_log_p[l], dtb_p[l])

        qc, kc, vc, ktc, yac, gzc, *gates_c = _pre(xc, cx[0], cx[1], *pre_args, seg=ctx_len, tm=ctx_len)
        res = _scan(qc, kc, vc, ktc, *gates_c, None, nb=nb, emit_o=ctx_out, emit_s=True)
        s_ctx = res[-1]
        ql, kl, vl, ktl, yal, gzl, *gates_l = _pre(xl, lat[0], lat[1], *pre_args, seg=GRID_W, tm=512)
        ofl, obl = _scan(ql, kl, vl, ktl, *gates_l, s_ctx, nb=nb, emit_o=True, emit_s=False)
        xl = _post(ofl, obl, gzl, yal, xl, lat[2], gn, w_out_b[l], fn,
                   tm=8 * GRID_W, transpose=True, final=(l == depth - 1))
        if ctx_out:
            xc = _post(res[0], res[1], gzc, yac, xc, cx[2], gn, w_out_b[l], fn,
                       tm=ctx_len, transpose=False, final=False)
    return xl
```
